```python
import jax, jax.numpy as jnp
from jax import lax
import numpy as np

D_MODEL = 1024
BATCH = 8
SEQ = 4096
DEPTH = 4

GLA_HEADS = 4
GLA_V = D_MODEL // 2
GLA_DV = GLA_V // GLA_HEADS
GLA_DK = GLA_DV // 2
GLA_QK = GLA_HEADS * GLA_DK
GATE_RANK = 16
GATE_NORMALIZER = 16.0
GLA_CHUNK = 64
POOL_WINDOWS = (2, 4, 8, 16)
POOL_WIDTH = D_MODEL // 4
POOL_GROUP_DIM = POOL_WIDTH // len(POOL_WINDOWS)
LRU_WIDTH = D_MODEL // 4
LRU_BLOCKS = 4
LRU_BLOCK_DIM = LRU_WIDTH // LRU_BLOCKS
LRU_CONV = 4
LRU_C = 8.0
D_MIX = GLA_V + POOL_WIDTH + LRU_WIDTH
IN_SPLITS = (GLA_QK, GLA_QK, GLA_V, GLA_V, GATE_RANK, POOL_WIDTH, LRU_WIDTH, LRU_WIDTH)
N_IN = 2 * GLA_QK + 2 * GLA_V + GATE_RANK + POOL_WIDTH + 2 * LRU_WIDTH
FFN_DIM = 3 * D_MODEL
FFN_CONV = 3
EPS = 1e-6

kernel_name = 'hybrid_gla_pool_rglru_block'


def rms_norm(x, g):
    xf = x.astype(jnp.float32)
    y = xf * lax.rsqrt(jnp.mean(xf * xf, axis=-1, keepdims=True) + EPS)
    return (y * g.astype(jnp.float32)).astype(x.dtype)


def causal_dwconv(x, w, b):
    K = w.shape[0]
    S = x.shape[1]
    xp = jnp.pad(x, ((0, 0), (K - 1, 0), (0, 0)))
    y = b
    for k in range(K):
        y = y + xp[:, k:k + S] * w[k]
    return y


def gla_chunked(q, k, v, log_alpha):
    B, S, _ = q.shape
    n_chunks = S // GLA_CHUNK

    def to_chunks(t, d):
        t = t.astype(jnp.float32).reshape(B, n_chunks, GLA_CHUNK, GLA_HEADS, d)
        return t.transpose(1, 0, 3, 2, 4)

    qc = to_chunks(q, GLA_DK) * (GLA_DK ** -0.5)
    kc = to_chunks(k, GLA_DK)
    vc = to_chunks(v, GLA_DV)
    bc = jnp.cumsum(to_chunks(log_alpha, GLA_DK), axis=3)
    causal = jnp.tril(jnp.ones((GLA_CHUNK, GLA_CHUNK), dtype=bool))[:, :, None]

    def step(state, inp):
        q_c, k_c, v_c, b_c = inp
        o_inter = jnp.einsum('bhcd,bhde->bhce', q_c * jnp.exp(b_c), state)
        diff = b_c[:, :, :, None, :] - b_c[:, :, None, :, :]
        decay = jnp.exp(jnp.where(causal, diff, -jnp.inf))
        scores = jnp.sum(q_c[:, :, :, None, :] * k_c[:, :, None, :, :] * decay, axis=-1)
        o_intra = jnp.einsum('bhij,bhje->bhie', scores, v_c)
        b_last = b_c[:, :, -1:, :]
        k_dec = k_c * jnp.exp(b_last - b_c)
        state = jnp.exp(b_last[:, :, 0, :, None]) * state + jnp.einsum('bhcd,bhce->bhde', k_dec, v_c)
        return state, o_inter + o_intra

    state0 = jnp.zeros((B, GLA_HEADS, GLA_DK, GLA_DV), jnp.float32)
    _, o = lax.scan(step, state0, (qc, kc, vc, bc))
    return o.transpose(1, 0, 3, 2, 4).reshape(B, S, GLA_HEADS, GLA_DV)


def pool_mixer(u, w, scale):
    B, S, _ = u.shape
    uf = u.astype(jnp.float32)
    c = jnp.cumsum(uf, axis=1)
    pos = jnp.arange(1, S + 1, dtype=jnp.float32)[:, None]
    outs = []
    for gi, win in enumerate(POOL_WINDOWS):
        sl = slice(gi * POOL_GROUP_DIM, (gi + 1) * POOL_GROUP_DIM)
        cg = c[..., sl]
        prev = jnp.pad(cg, ((0, 0), (win, 0), (0, 0)))[:, :S]
        mean = (cg - prev) / jnp.minimum(pos, float(win))
        d = (mean - uf[..., sl]).astype(u.dtype)
        outs.append(jnp.einsum('bsi,ij->bsj', d, w[gi]))
    return jnp.concatenate(outs, axis=-1) * scale


def rg_lru(xc, w_a, b_a, w_x, b_x, lam):
    B, S, _ = xc.shape
    xb = xc.reshape(B, S, LRU_BLOCKS, LRU_BLOCK_DIM)
    r = jax.nn.sigmoid(jnp.einsum('bshi,hij->bshj', xb, w_a).reshape(B, S, LRU_WIDTH) + b_a)
    i = jax.nn.sigmoid(jnp.einsum('bshi,hij->bshj', xb, w_x).reshape(B, S, LRU_WIDTH) + b_x)
    log_a = -LRU_C * r.astype(jnp.float32) * jax.nn.softplus(-lam.astype(jnp.float32))
    a = jnp.exp(log_a)
    u = jnp.sqrt(-jnp.expm1(2.0 * log_a)) * (i * xc).astype(jnp.float32)

    def combine(left, right):
        a1, b1 = left
        a2, b2 = right
        return a1 * a2, a2 * b1 + b2

    _, h = lax.associative_scan(combine, (a, u), axis=1)
    return h.astype(xc.dtype)


def setup_inputs(seed: int = 0) -> dict:
    key = jax.random.key(seed)
    ks = jax.random.split(key, 24)

    def nrm(k, shape, fan_in):
        return jax.random.normal(k, shape, jnp.float32) * (fan_in ** -0.5)

    def gain(k, shape):
        return 1.0 + 0.02 * jax.random.normal(k, shape, jnp.float32)

    def bias(k, shape):
        return 0.02 * jax.random.normal(k, shape, jnp.float32)

    a0 = jax.random.uniform(ks[14], (DEPTH, LRU_WIDTH), jnp.float32, minval=0.9, maxval=0.999)
    s = a0 ** (1.0 / LRU_C)
    lam = jnp.log(s) - jnp.log1p(-s)
    return {
        'x': jax.random.normal(ks[0], (BATCH, SEQ, D_MODEL), jnp.float32),
        'norm1_g': gain(ks[1], (DEPTH, D_MODEL)),
        'w_in': nrm(ks[2], (DEPTH, D_MODEL, N_IN), D_MODEL),
        'gla_wg2': nrm(ks[3], (DEPTH, GATE_RANK, GLA_QK), GATE_RANK),
        'gla_bg': bias(ks[4], (DEPTH, GLA_QK)),
        'gla_norm_g': gain(ks[5], (DEPTH, GLA_HEADS, GLA_DV)),
        'pool_w': nrm(ks[6], (DEPTH, len(POOL_WINDOWS), POOL_GROUP_DIM, POOL_GROUP_DIM), POOL_GROUP_DIM),
        'pool_scale': gain(ks[7], (DEPTH, POOL_WIDTH)),
        'lru_conv_w': nrm(ks[8], (DEPTH, LRU_CONV, LRU_WIDTH), LRU_CONV),
        'lru_conv_b': bias(ks[9], (DEPTH, LRU_WIDTH)),
        'lru_wa': nrm(ks[10], (DEPTH, LRU_BLOCKS, LRU_BLOCK_DIM, LRU_BLOCK_DIM), LRU_BLOCK_DIM),
        'lru_ba': bias(ks[11], (DEPTH, LRU_WIDTH)),
        'lru_wx': nrm(ks[12], (DEPTH, LRU_BLOCKS, LRU_BLOCK_DIM, LRU_BLOCK_DIM), LRU_BLOCK_DIM),
        'lru_bx': bias(ks[13], (DEPTH, LRU_WIDTH)),
        'lru_lambda': lam,
        'w_out': nrm(ks[15], (DEPTH, D_MIX, D_MODEL), D_MIX),
        'norm2_g': gain(ks[16], (DEPTH, D_MODEL)),
        'ffn_w_up': nrm(ks[17], (DEPTH, D_MODEL, 2 * FFN_DIM), D_MODEL),
        'ffn_conv_w': nrm(ks[18], (DEPTH, FFN_CONV, 2 * FFN_DIM), FFN_CONV),
        'ffn_conv_b': bias(ks[19], (DEPTH, 2 * FFN_DIM)),
        'ffn_w_down': nrm(ks[20], (DEPTH, FFN_DIM, D_MODEL), FFN_DIM),
        'final_g': gain(ks[21], (D_MODEL,)),
    }


def reference(x, norm1_g, w_in, gla_wg2, gla_bg, gla_norm_g, pool_w, pool_scale,
              lru_conv_w, lru_conv_b, lru_wa, lru_ba, lru_wx, lru_bx, lru_lambda,
              w_out, norm2_g, ffn_w_up, ffn_conv_w, ffn_conv_b, ffn_w_down, final_g):
    B, S, _ = x.shape
    split_points = [int(p) for p in np.cumsum(IN_SPLITS)[:-1]]
    for l in range(DEPTH):
        h = rms_norm(x, norm1_g[l])
        z = jnp.einsum('bsd,dn->bsn', h, w_in[l])
        q, k, v, g, g_low, pool_u, lru_x, lru_y = jnp.split(z, split_points, axis=-1)

        gate_logits = jnp.einsum('bsr,rk->bsk', g_low, gla_wg2[l]) + gla_bg[l]
        log_alpha = jax.nn.log_sigmoid(gate_logits.astype(jnp.float32)) / GATE_NORMALIZER
        o_gla = gla_chunked(q, k, v, log_alpha)
        o_gla = rms_norm(o_gla, gla_norm_g[l]).astype(x.dtype)
        o_gla = (o_gla * jax.nn.silu(g.reshape(B, S, GLA_HEADS, GLA_DV))).reshape(B, S, GLA_V)

        o_pool = pool_mixer(pool_u, pool_w[l], pool_scale[l])

        xc = causal_dwconv(lru_x, lru_conv_w[l], lru_conv_b[l])
        o_lru = rg_lru(xc, lru_wa[l], lru_ba[l], lru_wx[l], lru_bx[l], lru_lambda[l])
        o_lru = o_lru * jax.nn.gelu(lru_y, approximate=True)

        mix = jnp.concatenate([o_gla, o_pool, o_lru], axis=-1)
        x = x + jnp.einsum('bsm,md->bsd', mix, w_out[l])

        h2 = rms_norm(x, norm2_g[l])
        up = jnp.einsum('bsd,df->bsf', h2, ffn_w_up[l])
        up = causal_dwconv(up, ffn_conv_w[l], ffn_conv_b[l])
        gate, val = jnp.split(up, 2, axis=-1)
        x = x + jnp.einsum('bsf,fd->bsd', jax.nn.gelu(gate, approximate=True) * val, ffn_w_down[l])
    return rms_norm(x, final_g)
```

```python
import functools

import jax
import jax.numpy as jnp
from jax import lax
from jax.experimental import pallas as pl
from jax.experimental.pallas import tpu as pltpu

D_MODEL = 1024
GLA_HEADS = 4
GLA_DV = 128
GLA_DK = 64
GLA_QK = GLA_HEADS * GLA_DK
GLA_V = GLA_HEADS * GLA_DV
GATE_RANK = 16
GATE_NORMALIZER = 16.0
POOL_WINDOWS = (2, 4, 8, 16)
POOL_WIDTH = 256
POOL_GROUP_DIM = 64
LRU_WIDTH = 256
LRU_CONV = 4
LRU_C = 8.0
FFN_DIM = 3 * D_MODEL
FFN_CONV = 3
EPS = 1e-6

V7X_LANES = 128
V7X_SUBLANES = 8
V7X_VMEM_BYTES = 64 * 1024 * 1024

SEQ_TILE = 256
GLA_CHUNK = 64
FFN_CHUNK = 512
GATE_PAD = V7X_LANES
HIST = V7X_SUBLANES
POOL_HIST = 16

OFF_Q = 0
OFF_K = OFF_Q + GLA_QK
OFF_V = OFF_K + GLA_QK
OFF_G = OFF_V + GLA_V
OFF_POOL = OFF_G + GLA_V
OFF_LX = OFF_POOL + POOL_WIDTH
OFF_LY = OFF_LX + LRU_WIDTH
OFF_GL = OFF_LY + LRU_WIDTH
N_IN_PAD = OFF_GL + GATE_PAD


def _bf16(a):
    return a.astype(jnp.bfloat16)


def _dot(a, b):
    return jnp.dot(a, b, preferred_element_type=jnp.float32)


def _rms_norm(x, g):
    return x * lax.rsqrt(jnp.mean(x * x, axis=-1, keepdims=True) + EPS) * g


def _gelu_tanh(x):
    c = 0.7978845608028654
    return 0.5 * x * (1.0 + jnp.tanh(c * (x + 0.044715 * (x * x * x))))


def _sigmoid(x):
    return 1.0 / (1.0 + jnp.exp(-x))


def _log_sigmoid(x):
    return jnp.minimum(x, 0.0) - jnp.log(1.0 + jnp.exp(-jnp.abs(x)))


def _softplus(x):
    return jnp.maximum(x, 0.0) + jnp.log(1.0 + jnp.exp(-jnp.abs(x)))


def _shift_rows(xe, k):
    return pltpu.roll(xe, k, axis=0)


def _row_index(shape):
    return lax.broadcasted_iota(jnp.int32, shape, 0)


def _group_scan_add(x, group):
    pos = _row_index(x.shape) % group
    s = 1
    while s < group:
        x = x + jnp.where(pos >= s, _shift_rows(x, s), 0.0)
        s *= 2
    return x


def _linear_scan(a, u, h0):
    n = a.shape[0]
    pos = _row_index(a.shape) % V7X_SUBLANES
    s = 1
    while s < V7X_SUBLANES:
        keep = pos >= s
        u = u + a * jnp.where(keep, _shift_rows(u, s), 0.0)
        a = a * jnp.where(keep, _shift_rows(a, s), 1.0)
        s *= 2
    outs = []
    carry = h0
    for r in range(n // V7X_SUBLANES):
        rows = slice(r * V7X_SUBLANES, (r + 1) * V7X_SUBLANES)
        h = a[rows] * carry + u[rows]
        outs.append(h)
        carry = h[V7X_SUBLANES - 1:V7X_SUBLANES]
    return jnp.concatenate(outs, axis=0), carry


def _mixer_kernel(x_ref, n1_ref, win_ref, wg2_ref, bg_ref, gng_ref, pw_ref, ps_ref,
                  lcw_ref, lcb_ref, lw_ref, lb_ref, lam_ref, wout_ref,
                  o_ref,
                  state_ref, pool_hist_ref, lru_hist_ref, lru_h_ref):
    ts = x_ref.shape[1]
    seq_i = pl.program_id(1)

    @pl.when(seq_i == 0)
    def _():
        state_ref[...] = jnp.zeros_like(state_ref)
        pool_hist_ref[...] = jnp.zeros_like(pool_hist_ref)
        lru_hist_ref[...] = jnp.zeros_like(lru_hist_ref)
        lru_h_ref[...] = jnp.zeros_like(lru_h_ref)

    x = x_ref[0]
    h = _bf16(_rms_norm(x, n1_ref[...]))

    def proj(off, width):
        return _dot(h, win_ref[:, off:off + width])

    q = proj(OFF_Q, GLA_QK) * (GLA_DK ** -0.5)
    k = proj(OFF_K, GLA_QK)
    v = _bf16(proj(OFF_V, GLA_V))
    g_low = _bf16(proj(OFF_GL, GATE_PAD))
    logits = _dot(g_low, wg2_ref[...]) + bg_ref[...]
    log_alpha = _log_sigmoid(logits) * (1.0 / GATE_NORMALIZER)
    b = _group_scan_add(log_alpha, GLA_CHUNK)

    state = state_ref[...]
    causal = (lax.broadcasted_iota(jnp.int32, (GLA_CHUNK, GLA_CHUNK), 0)
              >= lax.broadcasted_iota(jnp.int32, (GLA_CHUNK, GLA_CHUNK), 1))
    o_chunks = []
    for c in range(ts // GLA_CHUNK):
        rows = slice(c * GLA_CHUNK, (c + 1) * GLA_CHUNK)
        bc = b[rows]
        qc = q[rows]
        kc = k[rows]
        vc = v[rows]
        b_last = bc[GLA_CHUNK - 1:GLA_CHUNK]
        b_mid = bc[GLA_CHUNK // 2 - 1:GLA_CHUNK // 2]
        q_inter = _bf16(qc * jnp.exp(bc))
        q_intra = _bf16(qc * jnp.exp(bc - b_mid))
        k_intra = _bf16(kc * jnp.exp(b_mid - bc))
        k_state = _bf16(kc * jnp.exp(b_last - bc))
        decay = jnp.exp(b_last)
        decay_col = jnp.broadcast_to(decay, (GLA_DV, GLA_QK)).T
        state_bf = _bf16(state)
        o_heads = []
        new_state = []
        for hd in range(GLA_HEADS):
            kl = slice(hd * GLA_DK, (hd + 1) * GLA_DK)
            vl = slice(hd * GLA_DV, (hd + 1) * GLA_DV)
            scores = lax.dot_general(q_intra[:, kl], k_intra[:, kl],
                                     (((1,), (1,)), ((), ())),
                                     preferred_element_type=jnp.float32)
            scores = _bf16(jnp.where(causal, scores, 0.0))
            o_h = _dot(scores, vc[:, vl]) + _dot(q_inter[:, kl], state_bf[kl, :])
            o_heads.append(o_h)
            upd = lax.dot_general(k_state[:, kl], vc[:, vl],
                                  (((0,), (0,)), ((), ())),
                                  preferred_element_type=jnp.float32)
            new_state.append(decay_col[kl, :] * state[kl, :] + upd)
        state = jnp.concatenate(new_state, axis=0)
        o_chunks.append(jnp.concatenate(o_heads, axis=1))
    state_ref[...] = state
    o_gla = jnp.concatenate(o_chunks, axis=0)

    gate = proj(OFF_G, GLA_V)
    gng = gng_ref[...]
    normed = []
    for hd in range(GLA_HEADS):
        vl = slice(hd * GLA_DV, (hd + 1) * GLA_DV)
        normed.append(_rms_norm(o_gla[:, vl], gng[:, vl]))
    o_gla = jnp.concatenate(normed, axis=1) * (gate * _sigmoid(gate))
    out = x + _dot(_bf16(o_gla), wout_ref[0:GLA_V, :])

    u = proj(OFF_POOL, POOL_WIDTH)
    ue = jnp.concatenate([pool_hist_ref[...], u], axis=0)
    pool_hist_ref[...] = u[ts - POOL_HIST:ts]
    s2 = ue + _shift_rows(ue, 1)
    s4 = s2 + _shift_rows(s2, 2)
    s8 = s4 + _shift_rows(s4, 4)
    s16 = s8 + _shift_rows(s8, 8)
    lane = lax.broadcasted_iota(jnp.int32, (ts, POOL_WIDTH), 1)
    grp = lane // POOL_GROUP_DIM
    trailing = jnp.where(grp == 0, s2[POOL_HIST:],
                         jnp.where(grp == 1, s4[POOL_HIST:],
                                   jnp.where(grp == 2, s8[POOL_HIST:], s16[POOL_HIST:])))
    win = jnp.where(grp == 0, 2.0, jnp.where(grp == 1, 4.0, jnp.where(grp == 2, 8.0, 16.0)))
    pos = (seq_i * ts + 1 + _row_index((ts, POOL_WIDTH))).astype(jnp.float32)
    mean = trailing / jnp.minimum(pos, win)
    o_pool = _dot(_bf16(mean - u), pw_ref[...]) * ps_ref[...]
    out = out + _dot(_bf16(o_pool), wout_ref[GLA_V:GLA_V + POOL_WIDTH, :])

    lx = proj(OFF_LX, LRU_WIDTH)
    xe = jnp.concatenate([lru_hist_ref[...], lx], axis=0)
    lru_hist_ref[...] = lx[ts - HIST:ts]
    lcw = lcw_ref[...]
    xc = lcb_ref[...] + xe * lcw[LRU_CONV - 1:LRU_CONV]
    for tap in range(1, LRU_CONV):
        xc = xc + _shift_rows(xe, tap) * lcw[LRU_CONV - 1 - tap:LRU_CONV - tap]
    xc = xc[HIST:]
    gates = _sigmoid(_dot(_bf16(xc), lw_ref[...]) + lb_ref[...])
    r_gate = gates[:, :LRU_WIDTH]
    i_gate = gates[:, LRU_WIDTH:]
    log_a = (-LRU_C) * r_gate * _softplus(-lam_ref[...])
    a = jnp.exp(log_a)
    th = jnp.tanh(log_a)
    u_in = jnp.sqrt(-2.0 * th / (1.0 - th)) * (i_gate * xc)
    h_prev = lru_h_ref[0:1, :]
    h_lru, h_last = _linear_scan(a, u_in, h_prev)
    lru_h_ref[...] = jnp.broadcast_to(h_last, lru_h_ref.shape)
    ly = proj(OFF_LY, LRU_WIDTH)
    o_lru = h_lru * _gelu_tanh(ly)
    out = out + _dot(_bf16(o_lru), wout_ref[GLA_V + POOL_WIDTH:, :])

    o_ref[0] = out


def _const_spec(shape):
    nd = len(shape)
    return pl.BlockSpec(shape, lambda b, s: (0,) * nd, pipeline_mode=pl.Buffered(1))


def _mixer_call(x, n1, win, wg2, bg, gng, pw, ps, lcw, lcb, lw, lb, lam, wout):
    B, S, D = x.shape
    ts = min(SEQ_TILE, S)
    tile = pl.BlockSpec((1, ts, D), lambda b, s: (b, s, 0))
    consts = (n1, win, wg2, bg, gng, pw, ps, lcw, lcb, lw, lb, lam, wout)
    return pl.pallas_call(
        _mixer_kernel,
        out_shape=jax.ShapeDtypeStruct(x.shape, x.dtype),
        grid=(B, S // ts),
        in_specs=[tile] + [_const_spec(c.shape) for c in consts],
        out_specs=tile,
        scratch_shapes=[
            pltpu.VMEM((GLA_QK, GLA_DV), jnp.float32),
            pltpu.VMEM((POOL_HIST, POOL_WIDTH), jnp.float32),
            pltpu.VMEM((HIST, LRU_WIDTH), jnp.float32),
            pltpu.VMEM((HIST, LRU_WIDTH), jnp.float32),
        ],
        compiler_params=pltpu.CompilerParams(
            dimension_semantics=("parallel", "arbitrary"),
            vmem_limit_bytes=V7X_VMEM_BYTES * 3 // 4),
        name="mixer",
    )(x, *consts)


def _ffn_kernel(x_ref, n2_ref, wg_ref, wv_ref, cwg_ref, cwv_ref, cbg_ref, cbv_ref, wd_ref,
                fg_ref, o_ref, hist_g_ref, hist_v_ref, *, final_norm):
    ts = x_ref.shape[1]

    @pl.when(pl.program_id(1) == 0)
    def _():
        hist_g_ref[...] = jnp.zeros_like(hist_g_ref)
        hist_v_ref[...] = jnp.zeros_like(hist_v_ref)

    x = x_ref[0]
    h = _bf16(_rms_norm(x, n2_ref[...]))

    def conv(up, hist_ref, cw_ref, cb_ref, cols):
        ue = jnp.concatenate([hist_ref[:, cols], up], axis=0)
        hist_ref[:, cols] = up[ts - HIST:ts]
        cw = cw_ref[:, cols]
        y = cb_ref[:, cols] + ue * cw[FFN_CONV - 1:FFN_CONV]
        for tap in range(1, FFN_CONV):
            y = y + _shift_rows(ue, tap) * cw[FFN_CONV - 1 - tap:FFN_CONV - tap]
        return y[HIST:]

    out = x
    for c in range(FFN_DIM // FFN_CHUNK):
        cols = slice(c * FFN_CHUNK, (c + 1) * FFN_CHUNK)
        gate = conv(_dot(h, wg_ref[:, cols]), hist_g_ref, cwg_ref, cbg_ref, cols)
        val = conv(_dot(h, wv_ref[:, cols]), hist_v_ref, cwv_ref, cbv_ref, cols)
        act = _bf16(_gelu_tanh(gate) * val)
        out = out + _dot(act, wd_ref[cols, :])
    if final_norm:
        out = _rms_norm(out, fg_ref[...])
    o_ref[0] = out


def _ffn_call(x, n2, wg, wv, cwg, cwv, cbg, cbv, wd, fg, final_norm):
    B, S, D = x.shape
    ts = min(SEQ_TILE, S)
    tile = pl.BlockSpec((1, ts, D), lambda b, s: (b, s, 0))
    consts = (n2, wg, wv, cwg, cwv, cbg, cbv, wd, fg)
    return pl.pallas_call(
        functools.partial(_ffn_kernel, final_norm=final_norm),
        out_shape=jax.ShapeDtypeStruct(x.shape, x.dtype),
        grid=(B, S // ts),
        in_specs=[tile] + [_const_spec(c.shape) for c in consts],
        out_specs=tile,
        scratch_shapes=[
            pltpu.VMEM((HIST, FFN_DIM), jnp.float32),
            pltpu.VMEM((HIST, FFN_DIM), jnp.float32),
        ],
        compiler_params=pltpu.CompilerParams(
            dimension_semantics=("parallel", "arbitrary"),
            vmem_limit_bytes=V7X_VMEM_BYTES * 3 // 4),
        name="ffn",
    )(x, *consts)


def _block_diag(w):
    G, n, m = w.shape
    eye = jnp.eye(G, dtype=w.dtype)
    return (eye[:, None, :, None] * w[:, :, None, :]).reshape(G * n, G * m)


def kernel(x, norm1_g, w_in, gla_wg2, gla_bg, gla_norm_g, pool_w, pool_scale, lru_conv_w, lru_conv_b, lru_wa, lru_ba, lru_wx, lru_bx, lru_lambda, w_out, norm2_g, ffn_w_up, ffn_conv_w, ffn_conv_b, ffn_w_down, final_g):
    depth = w_in.shape[0]
    row = lambda a: a.reshape(1, -1)
    o_q, o_k, o_v, o_g = 0, GLA_QK, 2 * GLA_QK, 2 * GLA_QK + GLA_V
    o_gl = o_g + GLA_V
    o_rest = o_gl + GATE_RANK
    for l in range(depth):
        wl = w_in[l]
        win = jnp.concatenate(
            [wl[:, o_q:o_gl], wl[:, o_rest:], wl[:, o_gl:o_rest],
             jnp.zeros((D_MODEL, GATE_PAD - GATE_RANK), wl.dtype)], axis=1)
        wg2 = jnp.concatenate(
            [gla_wg2[l], jnp.zeros((GATE_PAD - GATE_RANK, GLA_QK), gla_wg2.dtype)], axis=0)
        lw = jnp.concatenate([_block_diag(lru_wa[l]), _block_diag(lru_wx[l])], axis=1)
        lb = jnp.concatenate([lru_ba[l], lru_bx[l]])
        x = _mixer_call(
            x, row(norm1_g[l]), _bf16(win), _bf16(wg2), row(gla_bg[l]), row(gla_norm_g[l]),
            _bf16(_block_diag(pool_w[l])), row(pool_scale[l]),
            lru_conv_w[l], row(lru_conv_b[l]), _bf16(lw), row(lb), row(lru_lambda[l]),
            _bf16(w_out[l]))
        wup = ffn_w_up[l]
        x = _ffn_call(
            x, row(norm2_g[l]), _bf16(wup[:, :FFN_DIM]), _bf16(wup[:, FFN_DIM:]),
            ffn_conv_w[l][:, :FFN_DIM], ffn_conv_w[l][:, FFN_DIM:],
            row(ffn_conv_b[l][:FFN_DIM]), row(ffn_conv_b[l][FFN_DIM:]),
            _bf16(ffn_w_down[l]), row(final_g), final_norm=(l == depth - 1))
    return x
```

```python
import functools

import jax
import jax.numpy as jnp
from jax import lax
from jax.experimental import pallas as pl
from jax.experimental.pallas import tpu as pltpu

D_MODEL = 1024
GLA_HEADS = 4
GLA_DV = 128
GLA_DK = 64
GLA_QK = GLA_HEADS * GLA_DK
GLA_V = GLA_HEADS * GLA_DV
GATE_RANK = 16
GATE_NORMALIZER = 16.0
POOL_WINDOWS = (2, 4, 8, 16)
POOL_WIDTH = 256
POOL_GROUP_DIM = 64
LRU_WIDTH = 256
LRU_CONV = 4
LRU_C = 8.0
FFN_DIM = 3 * D_MODEL
FFN_CONV = 3
EPS = 1e-6

V7X_LANES = 128
V7X_SUBLANES = 8
V7X_VMEM_BYTES = 64 * 1024 * 1024

MIX_TILE = 256
FFN_TILE = 256
GLA_CHUNK = 64
FFN_CHUNK = 512
GATE_PAD = V7X_LANES
HIST = V7X_SUBLANES
POOL_HIST = 16
SPLIT_TERMS = 3

OFF_Q = 0
OFF_K = OFF_Q + GLA_QK
OFF_V = OFF_K + GLA_QK
OFF_G = OFF_V + GLA_V
OFF_POOL = OFF_G + GLA_V
OFF_LX = OFF_POOL + POOL_WIDTH
OFF_LY = OFF_LX + LRU_WIDTH
OFF_GL = OFF_LY + LRU_WIDTH
N_IN_PAD = OFF_GL + GATE_PAD


def _bf16(a):
    return a.astype(jnp.bfloat16)


def _dot(a, b):
    return jnp.dot(a, b, preferred_element_type=jnp.float32)


def _rms_norm(x, g):
    return x * lax.rsqrt(jnp.mean(x * x, axis=-1, keepdims=True) + EPS) * g


def _sigmoid(x):
    return 1.0 / (1.0 + jnp.exp(-x))


def _gelu_tanh(x):
    c2 = 2.0 * 0.7978845608028654
    return x * _sigmoid(x * (c2 + (c2 * 0.044715) * (x * x)))


def _log_sigmoid(x):
    return jnp.minimum(x, 0.0) - jnp.log(1.0 + jnp.exp(-jnp.abs(x)))


def _softplus(x):
    return jnp.maximum(x, 0.0) + jnp.log(1.0 + jnp.exp(-jnp.abs(x)))


def _to_groups(x):
    return x.reshape(x.shape[0] // V7X_SUBLANES, V7X_SUBLANES, x.shape[1])


def _from_groups(x):
    return x.reshape(x.shape[0] * V7X_SUBLANES, x.shape[2])


def _sublane_pos(shape):
    return lax.broadcasted_iota(jnp.int32, shape, 1)


def _shifted(xg, k):
    rot = pltpu.roll(xg, k, axis=1)
    pos = _sublane_pos(rot[1:].shape)
    return jnp.where(pos >= k, rot[1:], rot[:-1])


def _causal_conv(xg, w, bias):
    taps = w.shape[0]
    y = bias + xg[1:] * w[taps - 1:taps]
    for back in range(1, taps):
        y = y + _shifted(xg, back) * w[taps - 1 - back:taps - back]
    return y


def _linear_scan(a, u, h0):
    ag, ug = _to_groups(a), _to_groups(u)
    pos = _sublane_pos(ag.shape)
    s = 1
    while s < V7X_SUBLANES:
        keep = pos >= s
        ug = ug + ag * jnp.where(keep, pltpu.roll(ug, s, axis=1), 0.0)
        ag = ag * jnp.where(keep, pltpu.roll(ag, s, axis=1), 1.0)
        s *= 2
    outs = []
    carry = h0
    for r in range(ag.shape[0]):
        h = ag[r] * carry + ug[r]
        outs.append(h)
        carry = h[V7X_SUBLANES - 1:V7X_SUBLANES]
    return jnp.concatenate(outs, axis=0), carry


def _split_bf16(x, terms):
    parts = []
    for _ in range(terms - 1):
        p = _bf16(x)
        parts.append(p)
        x = x - p.astype(jnp.float32)
    parts.append(_bf16(x))
    return parts


def _mixer_kernel(x_ref, n1_ref, win_ref, wg2_ref, bg_ref, gng_ref, pw_ref, ps_ref,
                  lcw_ref, lcb_ref, lw_ref, lb_ref, lam_ref, wout_ref, tri_ref,
                  o_ref,
                  state_ref, pool_hist_ref, lru_hist_ref, lru_h_ref):
    ts = x_ref.shape[1]
    seq_i = pl.program_id(1)

    @pl.when(seq_i == 0)
    def _():
        state_ref[...] = jnp.zeros_like(state_ref)
        pool_hist_ref[...] = jnp.zeros_like(pool_hist_ref)
        lru_hist_ref[...] = jnp.zeros_like(lru_hist_ref)
        lru_h_ref[...] = jnp.zeros_like(lru_h_ref)

    x = x_ref[0]
    h = _bf16(_rms_norm(x, n1_ref[...]))

    def proj(off, width):
        return _dot(h, win_ref[:, off:off + width])

    q = proj(OFF_Q, GLA_QK)
    k = proj(OFF_K, GLA_QK)
    v = _bf16(proj(OFF_V, GLA_V))
    g_low = _bf16(proj(OFF_GL, GATE_PAD))
    logits = _dot(g_low, wg2_ref[...]) + bg_ref[...]
    log_alpha = _log_sigmoid(logits) * (1.0 / GATE_NORMALIZER)
    cums = _dot(tri_ref[...], jnp.concatenate(_split_bf16(log_alpha, SPLIT_TERMS), axis=1))
    b = cums[:, :GLA_QK]
    for t in range(1, SPLIT_TERMS):
        b = b + cums[:, t * GLA_QK:(t + 1) * GLA_QK]

    stack = GLA_HEADS * GLA_CHUNK
    head_mask = (lax.broadcasted_iota(jnp.int32, (stack, GLA_QK), 0) // GLA_CHUNK
                 == lax.broadcasted_iota(jnp.int32, (stack, GLA_QK), 1) // GLA_DK)
    causal = (lax.broadcasted_iota(jnp.int32, (stack, GLA_CHUNK), 0) % GLA_CHUNK
              >= lax.broadcasted_iota(jnp.int32, (stack, GLA_CHUNK), 1))

    def stack_heads(a):
        return _bf16(jnp.where(head_mask, jnp.concatenate([a] * GLA_HEADS, axis=0), 0.0))

    state = state_ref[...]
    o_chunks = []
    for c in range(ts // GLA_CHUNK):
        rows = slice(c * GLA_CHUNK, (c + 1) * GLA_CHUNK)
        bc, qc, kc, vc = b[rows], q[rows], k[rows], v[rows]
        b_last = bc[GLA_CHUNK - 1:GLA_CHUNK]
        b_mid = bc[GLA_CHUNK // 2 - 1:GLA_CHUNK // 2]
        q_inter = stack_heads(qc * jnp.exp(bc))
        q_intra = stack_heads(qc * jnp.exp(bc - b_mid))
        k_intra = _bf16(kc * jnp.exp(b_mid - bc))
        k_state = _bf16(kc * jnp.exp(b_last - bc))
        scores = lax.dot_general(q_intra, k_intra, (((1,), (1,)), ((), ())),
                                 preferred_element_type=jnp.float32)
        scores = _bf16(jnp.where(causal, scores, 0.0))
        o_inter = _dot(q_inter, _bf16(state))
        o_heads = []
        for hd in range(GLA_HEADS):
            hr = slice(hd * GLA_CHUNK, (hd + 1) * GLA_CHUNK)
            vl = slice(hd * GLA_DV, (hd + 1) * GLA_DV)
            o_heads.append(_dot(scores[hr], vc[:, vl]) + o_inter[hr])
        o_chunks.append(jnp.concatenate(o_heads, axis=1))
        upd_full = lax.dot_general(k_state, vc, (((0,), (0,)), ((), ())),
                                   preferred_element_type=jnp.float32)
        upd = jnp.concatenate(
            [upd_full[hd * GLA_DK:(hd + 1) * GLA_DK, hd * GLA_DV:(hd + 1) * GLA_DV]
             for hd in range(GLA_HEADS)], axis=0)
        decay_col = jnp.broadcast_to(jnp.exp(b_last), (GLA_DV, GLA_QK)).T
        state = decay_col * state + upd
    state_ref[...] = state
    o_gla = jnp.concatenate(o_chunks, axis=0)

    gate = proj(OFF_G, GLA_V)
    gng = gng_ref[...]
    normed = []
    for hd in range(GLA_HEADS):
        vl = slice(hd * GLA_DV, (hd + 1) * GLA_DV)
        normed.append(_rms_norm(o_gla[:, vl], gng[:, vl]))
    o_gla = jnp.concatenate(normed, axis=1) * (gate * _sigmoid(gate))
    out = x + _dot(_bf16(o_gla), wout_ref[0:GLA_V, :])

    u = proj(OFF_POOL, POOL_WIDTH)
    zero_group = jnp.zeros((1, V7X_SUBLANES, POOL_WIDTH), jnp.float32)
    ug = jnp.concatenate([zero_group, _to_groups(pool_hist_ref[...]), _to_groups(u)], axis=0)
    pool_hist_ref[...] = u[ts - POOL_HIST:ts]
    s2 = ug[1:] + _shifted(ug, 1)
    s4 = s2 + _shifted(jnp.concatenate([zero_group, s2], axis=0), 2)
    s8 = s4 + _shifted(jnp.concatenate([zero_group, s4], axis=0), 4)
    s16 = s8[1:] + s8[:-1]
    n_hist = POOL_HIST // V7X_SUBLANES
    grp = lax.broadcasted_iota(jnp.int32, (ts, POOL_WIDTH), 1) // POOL_GROUP_DIM
    trailing = jnp.where(grp == 0, _from_groups(s2[n_hist:]),
                         jnp.where(grp == 1, _from_groups(s4[n_hist:]),
                                   jnp.where(grp == 2, _from_groups(s8[n_hist:]),
                                             _from_groups(s16[n_hist - 1:]))))
    win = jnp.where(grp == 0, 2.0, jnp.where(grp == 1, 4.0, jnp.where(grp == 2, 8.0, 16.0)))
    pos = (seq_i * ts + 1 + lax.broadcasted_iota(jnp.int32, (ts, POOL_WIDTH), 0)).astype(jnp.float32)
    mean = trailing / jnp.minimum(pos, win)
    o_pool = _dot(_bf16(mean - u), pw_ref[...]) * ps_ref[...]
    out = out + _dot(_bf16(o_pool), wout_ref[GLA_V:GLA_V + POOL_WIDTH, :])

    lx = proj(OFF_LX, LRU_WIDTH)
    xg = jnp.concatenate([_to_groups(lru_hist_ref[...]), _to_groups(lx)], axis=0)
    lru_hist_ref[...] = lx[ts - HIST:ts]
    xc = _from_groups(_causal_conv(xg, lcw_ref[...], lcb_ref[...]))
    gates = _sigmoid(_dot(_bf16(xc), lw_ref[...]) + lb_ref[...])
    r_gate = gates[:, :LRU_WIDTH]
    i_gate = gates[:, LRU_WIDTH:]
    log_a = (-LRU_C) * r_gate * _softplus(-lam_ref[...])
    a = jnp.exp(log_a)
    th = jnp.tanh(log_a)
    u_in = jnp.sqrt(-2.0 * th / (1.0 - th)) * (i_gate * xc)
    h_lru, h_last = _linear_scan(a, u_in, lru_h_ref[0:1, :])
    lru_h_ref[...] = jnp.broadcast_to(h_last, lru_h_ref.shape)
    ly = proj(OFF_LY, LRU_WIDTH)
    o_lru = h_lru * _gelu_tanh(ly)
    out = out + _dot(_bf16(o_lru), wout_ref[GLA_V + POOL_WIDTH:, :])

    o_ref[0] = out


def _const_spec(shape):
    nd = len(shape)
    return pl.BlockSpec(shape, lambda b, s: (0,) * nd, pipeline_mode=pl.Buffered(1))


def _mixer_call(x, n1, win, wg2, bg, gng, pw, ps, lcw, lcb, lw, lb, lam, wout):
    B, S, D = x.shape
    ts = min(MIX_TILE, S)
    tile = pl.BlockSpec((1, ts, D), lambda b, s: (b, s, 0))
    pos = jnp.arange(ts)
    tri = ((pos[:, None] >= pos[None, :])
           & (pos[:, None] // GLA_CHUNK == pos[None, :] // GLA_CHUNK))
    tri = _bf16(tri.astype(jnp.float32))
    consts = (n1, win, wg2, bg, gng, pw, ps, lcw, lcb, lw, lb, lam, wout, tri)
    return pl.pallas_call(
        _mixer_kernel,
        out_shape=jax.ShapeDtypeStruct(x.shape, x.dtype),
        grid=(B, S // ts),
        in_specs=[tile] + [_const_spec(c.shape) for c in consts],
        out_specs=tile,
        scratch_shapes=[
            pltpu.VMEM((GLA_QK, GLA_DV), jnp.float32),
            pltpu.VMEM((POOL_HIST, POOL_WIDTH), jnp.float32),
            pltpu.VMEM((HIST, LRU_WIDTH), jnp.float32),
            pltpu.VMEM((HIST, LRU_WIDTH), jnp.float32),
        ],
        compiler_params=pltpu.CompilerParams(
            dimension_semantics=("parallel", "arbitrary"),
            vmem_limit_bytes=V7X_VMEM_BYTES * 3 // 4),
        name="mixer",
    )(x, *consts)


def _ffn_kernel(x_ref, n2_ref, wup_ref, cw_ref, cb_ref, wd_ref, fg_ref, o_ref,
                hist_ref, h_ref, act_ref, *, final_norm):
    ts = x_ref.shape[1]

    @pl.when(pl.program_id(1) == 0)
    def _():
        hist_ref[...] = jnp.zeros_like(hist_ref)

    x = x_ref[0]
    h_ref[...] = _bf16(_rms_norm(x, n2_ref[...]))

    for c in range(FFN_DIM // FFN_CHUNK):
        cols = slice(2 * c * FFN_CHUNK, 2 * (c + 1) * FFN_CHUNK)
        up = _dot(h_ref[...], wup_ref[:, cols])
        ug = jnp.concatenate([_to_groups(hist_ref[:, cols]), _to_groups(up)], axis=0)
        hist_ref[:, cols] = up[ts - HIST:ts]
        y = _from_groups(_causal_conv(ug, cw_ref[:, cols], cb_ref[:, cols]))
        act = _gelu_tanh(y[:, :FFN_CHUNK]) * y[:, FFN_CHUNK:]
        act_ref[:, c * FFN_CHUNK:(c + 1) * FFN_CHUNK] = _bf16(act)
    out = x + _dot(act_ref[...], wd_ref[...])
    if final_norm:
        out = _rms_norm(out, fg_ref[...])
    o_ref[0] = out


def _ffn_call(x, n2, wup, cw, cb, wd, fg, final_norm):
    B, S, D = x.shape
    ts = min(FFN_TILE, S)
    tile = pl.BlockSpec((1, ts, D), lambda b, s: (b, s, 0))
    consts = (n2, wup, cw, cb, wd, fg)
    return pl.pallas_call(
        functools.partial(_ffn_kernel, final_norm=final_norm),
        out_shape=jax.ShapeDtypeStruct(x.shape, x.dtype),
        grid=(B, S // ts),
        in_specs=[tile] + [_const_spec(c.shape) for c in consts],
        out_specs=tile,
        scratch_shapes=[
            pltpu.VMEM((HIST, 2 * FFN_DIM), jnp.float32),
            pltpu.VMEM((ts, D), jnp.bfloat16),
            pltpu.VMEM((ts, FFN_DIM), jnp.bfloat16),
        ],
        compiler_params=pltpu.CompilerParams(
            dimension_semantics=("parallel", "arbitrary"),
            vmem_limit_bytes=V7X_VMEM_BYTES * 3 // 4),
        name="ffn",
    )(x, *consts)


def _interleave_gate_val(a):
    lead = a.shape[:-1]
    n = FFN_DIM // FFN_CHUNK
    a = a.reshape(*lead, 2, n, FFN_CHUNK)
    return jnp.swapaxes(a, -3, -2).reshape(*lead, 2 * FFN_DIM)


def _block_diag(w):
    G, n, m = w.shape
    eye = jnp.eye(G, dtype=w.dtype)
    return (eye[:, None, :, None] * w[:, :, None, :]).reshape(G * n, G * m)


def kernel(x, norm1_g, w_in, gla_wg2, gla_bg, gla_norm_g, pool_w, pool_scale, lru_conv_w, lru_conv_b, lru_wa, lru_ba, lru_wx, lru_bx, lru_lambda, w_out, norm2_g, ffn_w_up, ffn_conv_w, ffn_conv_b, ffn_w_down, final_g):
    depth = w_in.shape[0]
    row = lambda a: a.reshape(1, -1)
    o_k, o_v, o_g = GLA_QK, 2 * GLA_QK, 2 * GLA_QK + GLA_V
    o_gl = o_g + GLA_V
    o_rest = o_gl + GATE_RANK
    for l in range(depth):
        wl = w_in[l]
        win = jnp.concatenate(
            [wl[:, :o_k] * (GLA_DK ** -0.5), wl[:, o_k:o_gl], wl[:, o_rest:], wl[:, o_gl:o_rest],
             jnp.zeros((D_MODEL, GATE_PAD - GATE_RANK), wl.dtype)], axis=1)
        wg2 = jnp.concatenate(
            [gla_wg2[l], jnp.zeros((GATE_PAD - GATE_RANK, GLA_QK), gla_wg2.dtype)], axis=0)
        lw = jnp.concatenate([_block_diag(lru_wa[l]), _block_diag(lru_wx[l])], axis=1)
        lb = jnp.concatenate([lru_ba[l], lru_bx[l]])
        x = _mixer_call(
            x, row(norm1_g[l]), _bf16(win), _bf16(wg2), row(gla_bg[l]), row(gla_norm_g[l]),
            _bf16(_block_diag(pool_w[l])), row(pool_scale[l]),
            lru_conv_w[l], row(lru_conv_b[l]), _bf16(lw), row(lb), row(lru_lambda[l]),
            _bf16(w_out[l]))
        x = _ffn_call(
            x, row(norm2_g[l]), _bf16(_interleave_gate_val(ffn_w_up[l])),
            _interleave_gate_val(ffn_conv_w[l]), row(_interleave_gate_val(ffn_conv_b[l])),
            _bf16(ffn_w_down[l]), row(final_g), final_norm=(l == depth - 1))
    return x
```

```python
import functools

import jax
import jax.numpy as jnp
from jax import lax
from jax.experimental import pallas as pl
from jax.experimental.pallas import tpu as pltpu

D_MODEL = 1024
GLA_HEADS = 4
GLA_DV = 128
GLA_DK = 64
GLA_QK = GLA_HEADS * GLA_DK
GLA_V = GLA_HEADS * GLA_DV
GATE_RANK = 16
GATE_NORMALIZER = 16.0
POOL_WINDOWS = (2, 4, 8, 16)
POOL_WIDTH = 256
POOL_GROUP_DIM = 64
LRU_WIDTH = 256
LRU_CONV = 4
LRU_C = 8.0
FFN_DIM = 3 * D_MODEL
FFN_CONV = 3
EPS = 1e-6

V7X_LANES = 128
V7X_SUBLANES = 8
V7X_VMEM_BYTES = 64 * 1024 * 1024

MIX_TILE = 512
TRI_ROWS = 256
FFN_TILE = 512
GLA_CHUNK = 64
FFN_CHUNK = 512
GATE_PAD = V7X_LANES
HIST = V7X_SUBLANES
POOL_HIST = 16
SPLIT_TERMS = 3

OFF_Q = 0
OFF_K = OFF_Q + GLA_QK
OFF_V = OFF_K + GLA_QK
OFF_G = OFF_V + GLA_V
OFF_POOL = OFF_G + GLA_V
OFF_LX = OFF_POOL + POOL_WIDTH
OFF_LY = OFF_LX + LRU_WIDTH
OFF_GL = OFF_LY + LRU_WIDTH
N_IN_PAD = OFF_GL + GATE_PAD


def _bf16(a):
    return a.astype(jnp.bfloat16)


def _dot(a, b):
    return jnp.dot(a, b, preferred_element_type=jnp.float32)


def _rms_norm(x, g):
    return x * lax.rsqrt(jnp.mean(x * x, axis=-1, keepdims=True) + EPS) * g


def _sigmoid(x):
    return 1.0 / (1.0 + jnp.exp(-x))


def _gelu_tanh(x):
    c2 = 2.0 * 0.7978845608028654
    return x * _sigmoid(x * (c2 + (c2 * 0.044715) * (x * x)))


def _log_sigmoid(x):
    return jnp.minimum(x, 0.0) - jnp.log(1.0 + jnp.exp(-jnp.abs(x)))


def _softplus(x):
    return jnp.maximum(x, 0.0) + jnp.log(1.0 + jnp.exp(-jnp.abs(x)))


def _to_groups(x):
    return x.reshape(x.shape[0] // V7X_SUBLANES, V7X_SUBLANES, x.shape[1])


def _from_groups(x):
    return x.reshape(x.shape[0] * V7X_SUBLANES, x.shape[2])


def _sublane_pos(shape):
    return lax.broadcasted_iota(jnp.int32, shape, 1)


def _shifted(xg, k):
    rot = pltpu.roll(xg, k, axis=1)
    pos = _sublane_pos(rot[1:].shape)
    return jnp.where(pos >= k, rot[1:], rot[:-1])


def _causal_conv(xg, w, bias):
    taps = w.shape[0]
    y = bias + xg[1:] * w[taps - 1:taps]
    for back in range(1, taps):
        y = y + _shifted(xg, back) * w[taps - 1 - back:taps - back]
    return y


def _linear_scan(a, u, h0):
    ag, ug = _to_groups(a), _to_groups(u)
    pos = _sublane_pos(ag.shape)
    s = 1
    while s < V7X_SUBLANES:
        keep = pos >= s
        ug = ug + ag * jnp.where(keep, pltpu.roll(ug, s, axis=1), 0.0)
        ag = ag * jnp.where(keep, pltpu.roll(ag, s, axis=1), 1.0)
        s *= 2
    outs = []
    carry = h0
    for r in range(ag.shape[0]):
        h = ag[r] * carry + ug[r]
        outs.append(h)
        carry = h[V7X_SUBLANES - 1:V7X_SUBLANES]
    return jnp.concatenate(outs, axis=0), carry


def _split_bf16(x, terms):
    parts = []
    for _ in range(terms - 1):
        p = _bf16(x)
        parts.append(p)
        x = x - p.astype(jnp.float32)
    parts.append(_bf16(x))
    return parts


def _mixer_kernel(x_ref, n1_ref, win_ref, wg2_ref, bg_ref, gng_ref, pw_ref, ps_ref,
                  lcw_ref, lcb_ref, lw_ref, lb_ref, lam_ref, wout_ref, tri_ref, hmask_ref,
                  o_ref,
                  state_ref, pool_hist_ref, lru_hist_ref, lru_h_ref):
    ts = x_ref.shape[1]
    seq_i = pl.program_id(1)

    @pl.when(seq_i == 0)
    def _():
        state_ref[...] = jnp.zeros_like(state_ref)
        pool_hist_ref[...] = jnp.zeros_like(pool_hist_ref)
        lru_hist_ref[...] = jnp.zeros_like(lru_hist_ref)
        lru_h_ref[...] = jnp.zeros_like(lru_h_ref)

    x = x_ref[0]
    h = _bf16(_rms_norm(x, n1_ref[...]))

    def proj(off, width):
        return _dot(h, win_ref[:, off:off + width])

    g_low = _bf16(proj(OFF_GL, GATE_PAD))
    q = proj(OFF_Q, GLA_QK)
    k = proj(OFF_K, GLA_QK)
    logits = _dot(g_low, wg2_ref[...]) + bg_ref[...]
    v = _bf16(proj(OFF_V, GLA_V))
    log_alpha = _log_sigmoid(logits) * (1.0 / GATE_NORMALIZER)
    parts = jnp.concatenate(_split_bf16(log_alpha, SPLIT_TERMS), axis=1)
    u = proj(OFF_POOL, POOL_WIDTH)
    lx = proj(OFF_LX, LRU_WIDTH)
    tri_rows = tri_ref.shape[0]
    cums = jnp.concatenate(
        [_dot(tri_ref[...], parts[r:r + tri_rows]) for r in range(0, ts, tri_rows)], axis=0)
    b = cums[:, :GLA_QK]
    for t in range(1, SPLIT_TERMS):
        b = b + cums[:, t * GLA_QK:(t + 1) * GLA_QK]

    zero_group = jnp.zeros((1, V7X_SUBLANES, POOL_WIDTH), jnp.float32)
    ug = jnp.concatenate([zero_group, _to_groups(pool_hist_ref[...]), _to_groups(u)], axis=0)
    pool_hist_ref[...] = u[ts - POOL_HIST:ts]
    s2 = ug[1:] + _shifted(ug, 1)
    s4 = s2 + _shifted(jnp.concatenate([zero_group, s2], axis=0), 2)
    s8 = s4 + _shifted(jnp.concatenate([zero_group, s4], axis=0), 4)
    s16 = s8[1:] + s8[:-1]
    n_hist = POOL_HIST // V7X_SUBLANES
    grp = lax.broadcasted_iota(jnp.int32, (ts, POOL_WIDTH), 1) // POOL_GROUP_DIM
    trailing = jnp.where(grp == 0, _from_groups(s2[n_hist:]),
                         jnp.where(grp == 1, _from_groups(s4[n_hist:]),
                                   jnp.where(grp == 2, _from_groups(s8[n_hist:]),
                                             _from_groups(s16[n_hist - 1:]))))
    win = jnp.where(grp == 0, 2.0, jnp.where(grp == 1, 4.0, jnp.where(grp == 2, 8.0, 16.0)))
    pos = (seq_i * ts + 1 + lax.broadcasted_iota(jnp.int32, (ts, POOL_WIDTH), 0)).astype(jnp.float32)
    pool_in = _bf16(trailing / jnp.minimum(pos, win) - u)

    xg = jnp.concatenate([_to_groups(lru_hist_ref[...]), _to_groups(lx)], axis=0)
    lru_hist_ref[...] = lx[ts - HIST:ts]
    xc = _from_groups(_causal_conv(xg, lcw_ref[...], lcb_ref[...]))

    stack = GLA_HEADS * GLA_CHUNK
    head_mask = hmask_ref[...]
    causal = (lax.broadcasted_iota(jnp.int32, (stack, GLA_CHUNK), 0) % GLA_CHUNK
              >= lax.broadcasted_iota(jnp.int32, (stack, GLA_CHUNK), 1))

    def stack_heads(a):
        return jnp.concatenate([_bf16(a)] * GLA_HEADS, axis=0) * head_mask

    def prep(c):
        rows = slice(c * GLA_CHUNK, (c + 1) * GLA_CHUNK)
        bc, qc, kc = b[rows], q[rows], k[rows]
        b_last = bc[GLA_CHUNK - 1:GLA_CHUNK]
        b_mid = bc[GLA_CHUNK // 2 - 1:GLA_CHUNK // 2]
        return dict(
            q_inter=stack_heads(qc * jnp.exp(bc)),
            q_intra=stack_heads(qc * jnp.exp(bc - b_mid)),
            k_intra=_bf16(kc * jnp.exp(b_mid - bc)),
            k_state=_bf16(kc * jnp.exp(b_last - bc)),
            decay_col=jnp.broadcast_to(jnp.exp(b_last), (GLA_DV, GLA_QK)).T,
            v=v[rows])

    def score_dot(p):
        return lax.dot_general(p["q_intra"], p["k_intra"], (((1,), (1,)), ((), ())),
                               preferred_element_type=jnp.float32)

    n_chunks = ts // GLA_CHUNK
    fillers = {}
    fillers[0] = lambda: dict(o_pool=_dot(pool_in, pw_ref[...]) * ps_ref[...])
    fillers[1] = lambda: dict(gates=_dot(_bf16(xc), lw_ref[...]) + lb_ref[...])
    fillers[2] = lambda: dict(ly=proj(OFF_LY, LRU_WIDTH))
    fillers[3] = lambda: dict(gate=proj(OFF_G, GLA_V))
    filled = {}

    state = state_ref[...]
    o_chunks = []
    cur = prep(0)
    scores = score_dot(cur)
    for c in range(n_chunks):
        if c + 1 < n_chunks:
            nxt = prep(c + 1)
            scores_next = score_dot(nxt)
        o_inter = _dot(cur["q_inter"], _bf16(state))
        upd_full = lax.dot_general(cur["k_state"], cur["v"], (((0,), (0,)), ((), ())),
                                   preferred_element_type=jnp.float32)
        if c in fillers:
            filled.update(fillers[c]())
        scores_m = _bf16(jnp.where(causal, scores, 0.0))
        o_heads = []
        for hd in range(GLA_HEADS):
            hr = slice(hd * GLA_CHUNK, (hd + 1) * GLA_CHUNK)
            vl = slice(hd * GLA_DV, (hd + 1) * GLA_DV)
            o_heads.append(_dot(scores_m[hr], cur["v"][:, vl]) + o_inter[hr])
        o_chunks.append(jnp.concatenate(o_heads, axis=1))
        upd = jnp.concatenate(
            [upd_full[hd * GLA_DK:(hd + 1) * GLA_DK, hd * GLA_DV:(hd + 1) * GLA_DV]
             for hd in range(GLA_HEADS)], axis=0)
        state = cur["decay_col"] * state + upd
        if c + 1 < n_chunks:
            cur, scores = nxt, scores_next
    state_ref[...] = state
    for c in sorted(fillers):
        if c >= n_chunks:
            filled.update(fillers[c]())
    o_gla = jnp.concatenate(o_chunks, axis=0)

    out = x + _dot(_bf16(filled["o_pool"]), wout_ref[GLA_V:GLA_V + POOL_WIDTH, :])

    gates = _sigmoid(filled["gates"])
    r_gate = gates[:, :LRU_WIDTH]
    i_gate = gates[:, LRU_WIDTH:]
    log_a = (-LRU_C) * r_gate * _softplus(-lam_ref[...])
    a = jnp.exp(log_a)
    th = jnp.tanh(log_a)
    u_in = jnp.sqrt(-2.0 * th / (1.0 - th)) * (i_gate * xc)
    h_lru, h_last = _linear_scan(a, u_in, lru_h_ref[0:1, :])
    lru_h_ref[...] = jnp.broadcast_to(h_last, lru_h_ref.shape)
    o_lru = h_lru * _gelu_tanh(filled["ly"])
    out = out + _dot(_bf16(o_lru), wout_ref[GLA_V + POOL_WIDTH:, :])

    gate = filled["gate"]
    gng = gng_ref[...]
    normed = []
    for hd in range(GLA_HEADS):
        vl = slice(hd * GLA_DV, (hd + 1) * GLA_DV)
        normed.append(_rms_norm(o_gla[:, vl], gng[:, vl]))
    o_gla = jnp.concatenate(normed, axis=1) * (gate * _sigmoid(gate))
    out = out + _dot(_bf16(o_gla), wout_ref[0:GLA_V, :])

    o_ref[0] = out


def _const_spec(shape):
    nd = len(shape)
    return pl.BlockSpec(shape, lambda b, s: (0,) * nd, pipeline_mode=pl.Buffered(1))


def _mixer_call(x, n1, win, wg2, bg, gng, pw, ps, lcw, lcb, lw, lb, lam, wout):
    B, S, D = x.shape
    ts = min(MIX_TILE, S)
    tile = pl.BlockSpec((1, ts, D), lambda b, s: (b, s, 0))
    pos = jnp.arange(min(ts, TRI_ROWS))
    tri = ((pos[:, None] >= pos[None, :])
           & (pos[:, None] // GLA_CHUNK == pos[None, :] // GLA_CHUNK))
    tri = _bf16(tri.astype(jnp.float32))
    stack = GLA_HEADS * GLA_CHUNK
    hmask = _bf16((jnp.arange(stack)[:, None] // GLA_CHUNK
                   == jnp.arange(GLA_QK)[None, :] // GLA_DK).astype(jnp.float32))
    consts = (n1, win, wg2, bg, gng, pw, ps, lcw, lcb, lw, lb, lam, wout, tri, hmask)
    return pl.pallas_call(
        _mixer_kernel,
        out_shape=jax.ShapeDtypeStruct(x.shape, x.dtype),
        grid=(B, S // ts),
        in_specs=[tile] + [_const_spec(c.shape) for c in consts],
        out_specs=tile,
        scratch_shapes=[
            pltpu.VMEM((GLA_QK, GLA_DV), jnp.float32),
            pltpu.VMEM((POOL_HIST, POOL_WIDTH), jnp.float32),
            pltpu.VMEM((HIST, LRU_WIDTH), jnp.float32),
            pltpu.VMEM((HIST, LRU_WIDTH), jnp.float32),
        ],
        compiler_params=pltpu.CompilerParams(
            dimension_semantics=("parallel", "arbitrary"),
            vmem_limit_bytes=V7X_VMEM_BYTES * 3 // 4),
        name="mixer",
    )(x, *consts)


def _ffn_kernel(x_ref, n2_ref, wup_ref, cw_ref, cb_ref, wd_ref, fg_ref, o_ref,
                hist_ref, *, final_norm):
    ts = x_ref.shape[1]

    @pl.when(pl.program_id(1) == 0)
    def _():
        hist_ref[...] = jnp.zeros_like(hist_ref)

    x = x_ref[0]
    h = _bf16(_rms_norm(x, n2_ref[...]))

    def conv(up, cols):
        ug = jnp.concatenate([_to_groups(hist_ref[:, cols]), _to_groups(up)], axis=0)
        hist_ref[:, cols] = up[ts - HIST:ts]
        return _from_groups(_causal_conv(ug, cw_ref[:, cols], cb_ref[:, cols]))

    def up_proj(c):
        gate_cols = slice(c * FFN_CHUNK, (c + 1) * FFN_CHUNK)
        val_cols = slice(FFN_DIM + c * FFN_CHUNK, FFN_DIM + (c + 1) * FFN_CHUNK)
        return (gate_cols, _dot(h, wup_ref[:, gate_cols])), (val_cols, _dot(h, wup_ref[:, val_cols]))

    n_chunks = FFN_DIM // FFN_CHUNK
    out = x
    nxt = up_proj(0)
    for c in range(n_chunks):
        (gate_cols, up_gate), (val_cols, up_val) = nxt
        if c + 1 < n_chunks:
            nxt = up_proj(c + 1)
        act = _bf16(_gelu_tanh(conv(up_gate, gate_cols)) * conv(up_val, val_cols))
        out = out + _dot(act, wd_ref[c * FFN_CHUNK:(c + 1) * FFN_CHUNK, :])
    if final_norm:
        out = _rms_norm(out, fg_ref[...])
    o_ref[0] = out


def _ffn_call(x, n2, wup, cw, cb, wd, fg, final_norm):
    B, S, D = x.shape
    ts = min(FFN_TILE, S)
    tile = pl.BlockSpec((1, ts, D), lambda b, s: (b, s, 0))
    consts = (n2, wup, cw, cb, wd, fg)
    return pl.pallas_call(
        functools.partial(_ffn_kernel, final_norm=final_norm),
        out_shape=jax.ShapeDtypeStruct(x.shape, x.dtype),
        grid=(B, S // ts),
        in_specs=[tile] + [_const_spec(c.shape) for c in consts],
        out_specs=tile,
        scratch_shapes=[
            pltpu.VMEM((HIST, 2 * FFN_DIM), jnp.float32),
        ],
        compiler_params=pltpu.CompilerParams(
            dimension_semantics=("parallel", "arbitrary"),
            vmem_limit_bytes=V7X_VMEM_BYTES * 3 // 4),
        name="ffn",
    )(x, *consts)


def _block_diag(w):
    G, n, m = w.shape
    eye = jnp.eye(G, dtype=w.dtype)
    return (eye[:, None, :, None] * w[:, :, None, :]).reshape(G * n, G * m)


def kernel(x, norm1_g, w_in, gla_wg2, gla_bg, gla_norm_g, pool_w, pool_scale, lru_conv_w, lru_conv_b, lru_wa, lru_ba, lru_wx, lru_bx, lru_lambda, w_out, norm2_g, ffn_w_up, ffn_conv_w, ffn_conv_b, ffn_w_down, final_g):
    depth = w_in.shape[0]
    row = lambda a: a.reshape(1, -1)
    o_k, o_v, o_g = GLA_QK, 2 * GLA_QK, 2 * GLA_QK + GLA_V
    o_gl = o_g + GLA_V
    o_rest = o_gl + GATE_RANK
    for l in range(depth):
        wl = w_in[l]
        win = jnp.concatenate(
            [wl[:, :o_k] * (GLA_DK ** -0.5), wl[:, o_k:o_gl], wl[:, o_rest:], wl[:, o_gl:o_rest],
             jnp.zeros((D_MODEL, GATE_PAD - GATE_RANK), wl.dtype)], axis=1)
        wg2 = jnp.concatenate(
            [gla_wg2[l], jnp.zeros((GATE_PAD - GATE_RANK, GLA_QK), gla_wg2.dtype)], axis=0)
        lw = jnp.concatenate([_block_diag(lru_wa[l]), _block_diag(lru_wx[l])], axis=1)
        lb = jnp.concatenate([lru_ba[l], lru_bx[l]])
        x = _mixer_call(
            x, row(norm1_g[l]), _bf16(win), _bf16(wg2), row(gla_bg[l]), row(gla_norm_g[l]),
            _bf16(_block_diag(pool_w[l])), row(pool_scale[l]),
            lru_conv_w[l], row(lru_conv_b[l]), _bf16(lw), row(lb), row(lru_lambda[l]),
            _bf16(w_out[l]))
        x = _ffn_call(
            x, row(norm2_g[l]), _bf16(ffn_w_up[l]), ffn_conv_w[l], row(ffn_conv_b[l]),
            _bf16(ffn_w_down[l]), row(final_g), final_norm=(l == depth - 1))
    return x
```
